```python
import jax, jax.numpy as jnp
from jax import lax
import numpy as np

D_MODEL = 2048
BATCH = 8
SEQ = 2048
DEPTH = 1
DEC_BATCH = 4
DEC_SEQ = 8192
PAST_LEN = 128

MIX_WIDTH = D_MODEL
POOL_WIDTH = MIX_WIDTH // 2
CONV_WIDTH = MIX_WIDTH - POOL_WIDTH
POOL_WINDOWS = (2, 4, 8, 16)
N_POOL_GROUPS = len(POOL_WINDOWS)
POOL_GROUP = POOL_WIDTH // N_POOL_GROUPS
CONV_HEADS = 8
CONV_K = 3
IN_WIDTH = POOL_WIDTH + 3 * CONV_WIDTH
D_FF = ((8 * D_MODEL // 3 + 127) // 128) * 128
N_SUBLAYERS = 3
N_MOD = 3 * N_SUBLAYERS
ALPHA = (2.0 * DEPTH) ** 0.25
BETA = (8.0 * DEPTH) ** -0.25
LN_EPS = 1e-5

kernel_name = "hybrid_pool_shortconv_macaron_adaln_encoder"


def layer_norm(x, g, b):
    xf = x.astype(jnp.float32)
    mu = jnp.mean(xf, axis=-1, keepdims=True)
    xc = xf - mu
    var = jnp.mean(jnp.square(xc), axis=-1, keepdims=True)
    y = xc * lax.rsqrt(var + LN_EPS)
    return (y * g.astype(jnp.float32) + b.astype(jnp.float32)).astype(x.dtype)


def modulate(x, shift, scale):
    return x * (1.0 + scale[:, None, :]) + shift[:, None, :]


def swiglu(h, w1, w3, w2):
    return (jax.nn.silu(h @ w1) * (h @ w3)) @ w2


def centred_mean_minus_self(z, window):
    bsz, seq, ch = z.shape
    zf = z.astype(jnp.float32)
    cs = jnp.concatenate([jnp.zeros((bsz, 1, ch), jnp.float32), jnp.cumsum(zf, axis=1)], axis=1)
    t = np.arange(seq)
    lo = np.clip(t - window // 2, 0, seq)
    hi = np.clip(t + window // 2, 0, seq)
    cnt = (hi - lo).astype(np.float32)
    mean = (cs[:, hi] - cs[:, lo]) / cnt[None, :, None]
    return (mean - zf).astype(z.dtype)


def pool_mixer(u, w_pool, s_pool):
    bsz, seq, _ = u.shape
    ug = u.reshape(bsz, seq, N_POOL_GROUPS, POOL_GROUP)
    pooled = jnp.stack([centred_mean_minus_self(ug[:, :, g], POOL_WINDOWS[g]) for g in range(N_POOL_GROUPS)], axis=2)
    pooled = jnp.einsum('bsgc,gcd->bsgd', pooled, w_pool)
    return pooled.reshape(bsz, seq, POOL_WIDTH) * s_pool


def short_conv_mixer(b_gate, c_gate, v, w_conv):
    seq = v.shape[1]
    z = c_gate * v
    pad = CONV_K // 2
    zp = jnp.pad(z, ((0, 0), (pad, pad), (0, 0)))
    y = zp[:, 0:seq] * w_conv[0]
    for k in range(1, CONV_K):
        y = y + zp[:, k:k + seq] * w_conv[k]
    return b_gate * y


def encoder_layer(x, c, w_ada, b_ada, ffn1_w1, ffn1_w3, ffn1_w2, w_in, w_pool, s_pool,
                  w_conv, w_out, ffn2_w1, ffn2_w3, ffn2_w2, ln_g, ln_b):
    bsz = x.shape[0]
    mod = (jax.nn.silu(c) @ w_ada + b_ada).reshape(bsz, N_MOD, D_MODEL)
    sh1, sc1, g1, sh2, sc2, g2, sh3, sc3, g3 = [mod[:, i] for i in range(N_MOD)]

    f1 = swiglu(modulate(x, sh1, sc1), ffn1_w1, ffn1_w3, ffn1_w2)
    x = layer_norm(ALPHA * x + 0.5 * (1.0 + g1)[:, None, :] * f1, ln_g[0], ln_b[0])

    proj = modulate(x, sh2, sc2) @ w_in
    u_pool = proj[..., :POOL_WIDTH]
    b_gate = proj[..., POOL_WIDTH:POOL_WIDTH + CONV_WIDTH]
    c_gate = proj[..., POOL_WIDTH + CONV_WIDTH:POOL_WIDTH + 2 * CONV_WIDTH]
    v = proj[..., POOL_WIDTH + 2 * CONV_WIDTH:]
    heads = jnp.concatenate([pool_mixer(u_pool, w_pool, s_pool),
                             short_conv_mixer(b_gate, c_gate, v, w_conv)], axis=-1)
    m = heads @ w_out
    x = layer_norm(ALPHA * x + (1.0 + g2)[:, None, :] * m, ln_g[1], ln_b[1])

    f2 = swiglu(modulate(x, sh3, sc3), ffn2_w1, ffn2_w3, ffn2_w2)
    x = layer_norm(ALPHA * x + 0.5 * (1.0 + g3)[:, None, :] * f2, ln_g[2], ln_b[2])
    return x


def trunk(x, c, w_ada, b_ada, ffn1_w1, ffn1_w3, ffn1_w2, w_in, w_pool, s_pool,
          w_conv, w_out, ffn2_w1, ffn2_w3, ffn2_w2, ln_g, ln_b):
    for l in range(DEPTH):
        x = encoder_layer(x, c, w_ada[l], b_ada[l], ffn1_w1[l], ffn1_w3[l], ffn1_w2[l],
                          w_in[l], w_pool[l], s_pool[l], w_conv[l], w_out[l],
                          ffn2_w1[l], ffn2_w3[l], ffn2_w2[l], ln_g[l], ln_b[l])
    return x


def setup_inputs(seed: int = 0) -> dict:
    key = jax.random.key(seed)
    ks = jax.random.split(key, 20)
    f32 = jnp.float32
    nrm = lambda k, shape, std: jax.random.normal(k, shape, f32) * std
    return {
        "x_prompt": nrm(ks[0], (BATCH, SEQ, D_MODEL), 1.0),
        "x_sample": nrm(ks[1], (DEC_BATCH, DEC_SEQ, D_MODEL), 1.0),
        "c_prompt": nrm(ks[2], (BATCH, D_MODEL), 1.0),
        "c_sample": nrm(ks[3], (DEC_BATCH, D_MODEL), 1.0),
        "w_ada": nrm(ks[4], (DEPTH, D_MODEL, N_MOD * D_MODEL), 0.5 * D_MODEL ** -0.5),
        "b_ada": nrm(ks[5], (DEPTH, N_MOD * D_MODEL), 0.01),
        "ffn1_w1": nrm(ks[6], (DEPTH, D_MODEL, D_FF), D_MODEL ** -0.5),
        "ffn1_w3": nrm(ks[7], (DEPTH, D_MODEL, D_FF), D_MODEL ** -0.5),
        "ffn1_w2": nrm(ks[8], (DEPTH, D_FF, D_MODEL), BETA * D_FF ** -0.5),
        "w_in": nrm(ks[9], (DEPTH, D_MODEL, IN_WIDTH), D_MODEL ** -0.5),
        "w_pool": nrm(ks[10], (DEPTH, N_POOL_GROUPS, POOL_GROUP, POOL_GROUP), POOL_GROUP ** -0.5),
        "s_pool": 1.0 + nrm(ks[11], (DEPTH, POOL_WIDTH), 0.02),
        "w_conv": nrm(ks[12], (DEPTH, CONV_K, CONV_WIDTH), CONV_K ** -0.5),
        "w_out": nrm(ks[13], (DEPTH, MIX_WIDTH, D_MODEL), BETA * MIX_WIDTH ** -0.5),
        "ffn2_w1": nrm(ks[14], (DEPTH, D_MODEL, D_FF), D_MODEL ** -0.5),
        "ffn2_w3": nrm(ks[15], (DEPTH, D_MODEL, D_FF), D_MODEL ** -0.5),
        "ffn2_w2": nrm(ks[16], (DEPTH, D_FF, D_MODEL), BETA * D_FF ** -0.5),
        "ln_g": 1.0 + nrm(ks[17], (DEPTH, N_SUBLAYERS, D_MODEL), 0.02),
        "ln_b": nrm(ks[18], (DEPTH, N_SUBLAYERS, D_MODEL), 0.02),
    }


def reference(x_prompt, x_sample, c_prompt, c_sample, w_ada, b_ada, ffn1_w1, ffn1_w3, ffn1_w2,
              w_in, w_pool, s_pool, w_conv, w_out, ffn2_w1, ffn2_w3, ffn2_w2, ln_g, ln_b):
    y_prompt = trunk(x_prompt, c_prompt, w_ada, b_ada, ffn1_w1, ffn1_w3, ffn1_w2, w_in, w_pool,
                     s_pool, w_conv, w_out, ffn2_w1, ffn2_w3, ffn2_w2, ln_g, ln_b)
    y_sample = trunk(x_sample, c_sample, w_ada, b_ada, ffn1_w1, ffn1_w3, ffn1_w2, w_in, w_pool,
                     s_pool, w_conv, w_out, ffn2_w1, ffn2_w3, ffn2_w2, ln_g, ln_b)
    return (y_prompt, y_sample)
```

```python
import functools

import jax
import jax.numpy as jnp
from jax import lax
from jax.experimental import pallas as pl
from jax.experimental.pallas import tpu as pltpu

F32 = jnp.float32
BF16 = jnp.bfloat16

POOL_WINDOWS = (2, 4, 8, 16)
CONV_K = 3
N_MOD = 9
LN_EPS = 1e-5

V7X_VMEM_BYTES = 64 * 1024 * 1024
V7X_F32_SUBLANES = 8
V7X_LANES = 128

FFN_TOKENS = 512
FFN_HIDDEN = 512
MIX_TOKENS = 512
HALO = V7X_F32_SUBLANES
ADA_COLS = 1024
VMEM_LIMIT = 60 * 1024 * 1024

assert HALO >= max(POOL_WINDOWS) // 2 and HALO >= CONV_K // 2


def _layer_norm(r, g, b):
    mu = jnp.mean(r, axis=-1, keepdims=True)
    xc = r - mu
    var = jnp.mean(xc * xc, axis=-1, keepdims=True)
    return xc * lax.rsqrt(var + LN_EPS) * g + b


def _ada_kernel(c_ref, w_ref, b_ref, o_ref):
    c = c_ref[...]
    h = (c * jax.nn.sigmoid(c)).astype(BF16)
    o_ref[...] = jnp.dot(h, w_ref[...].astype(BF16), preferred_element_type=F32) + b_ref[...]


def _ada_ln(c, w_ada, b_ada):
    rows, d = c.shape
    n = w_ada.shape[1]
    return pl.pallas_call(
        _ada_kernel,
        grid=(n // ADA_COLS,),
        in_specs=[
            pl.BlockSpec((rows, d), lambda j: (0, 0)),
            pl.BlockSpec((d, ADA_COLS), lambda j: (0, j)),
            pl.BlockSpec((1, ADA_COLS), lambda j: (0, j)),
        ],
        out_specs=pl.BlockSpec((rows, ADA_COLS), lambda j: (0, j)),
        out_shape=jax.ShapeDtypeStruct((rows, n), F32),
        compiler_params=pltpu.CompilerParams(
            dimension_semantics=("arbitrary",), vmem_limit_bytes=VMEM_LIMIT),
        name="ada_ln",
    )(c, w_ada, b_ada.reshape(1, n))


def _ffn_kernel(x_ref, mod_ref, w1_ref, w3_ref, w2_ref, lng_ref, lnb_ref, o_ref,
                h_ref, acc_ref, *, mod_base, alpha):
    j = pl.program_id(2)

    @pl.when(j == 0)
    def _():
        shift = mod_ref[mod_base:mod_base + 1, :]
        scale = mod_ref[mod_base + 1:mod_base + 2, :]
        h_ref[...] = (x_ref[...] * (1.0 + scale) + shift).astype(BF16)

    h = h_ref[...]
    a = jnp.dot(h, w1_ref[...], preferred_element_type=F32)
    b = jnp.dot(h, w3_ref[...], preferred_element_type=F32)
    hid = (a * jax.nn.sigmoid(a) * b).astype(BF16)
    p = jnp.dot(hid, w2_ref[...], preferred_element_type=F32)

    @pl.when(j == 0)
    def _():
        acc_ref[...] = p

    @pl.when(j > 0)
    def _():
        acc_ref[...] += p

    @pl.when(j == pl.num_programs(2) - 1)
    def _():
        gate = 0.5 * (1.0 + mod_ref[mod_base + 2:mod_base + 3, :])
        r = alpha * x_ref[...] + gate * acc_ref[...]
        o_ref[...] = _layer_norm(r, lng_ref[...], lnb_ref[...])


def _ffn_sublayer(x, mod, w1, w3, w2, ln_g, ln_b, *, mod_base, alpha):
    bsz, seq, d = x.shape
    fp = w1.shape[1]
    tm, tf = FFN_TOKENS, FFN_HIDDEN
    kern = functools.partial(_ffn_kernel, mod_base=mod_base, alpha=alpha)
    return pl.pallas_call(
        kern,
        grid=(bsz, seq // tm, fp // tf),
        in_specs=[
            pl.BlockSpec((None, tm, d), lambda b, s, j: (b, s, 0)),
            pl.BlockSpec((None, N_MOD, d), lambda b, s, j: (b, 0, 0)),
            pl.BlockSpec((d, tf), lambda b, s, j: (0, j)),
            pl.BlockSpec((d, tf), lambda b, s, j: (0, j)),
            pl.BlockSpec((tf, d), lambda b, s, j: (j, 0)),
            pl.BlockSpec((1, d), lambda b, s, j: (0, 0)),
            pl.BlockSpec((1, d), lambda b, s, j: (0, 0)),
        ],
        out_specs=pl.BlockSpec((None, tm, d), lambda b, s, j: (b, s, 0)),
        out_shape=jax.ShapeDtypeStruct(x.shape, F32),
        scratch_shapes=[pltpu.VMEM((tm, d), BF16), pltpu.VMEM((tm, d), F32)],
        compiler_params=pltpu.CompilerParams(
            dimension_semantics=("parallel", "parallel", "arbitrary"),
            vmem_limit_bytes=VMEM_LIMIT),
        name="ffn",
    )(x, mod, w1, w3, w2, ln_g, ln_b)


def _mixer_kernel(prev_ref, x_ref, next_ref, mod_ref, w_in_ref, w_pool_ref, s_pool_ref,
                  w_conv_ref, w_out_ref, lng_ref, lnb_ref, o_ref,
                  heads_ref, slab_ref, *, seq, alpha):
    tm = x_ref.shape[0]
    rows = tm + 2 * HALO
    pool_w = w_pool_ref.shape[0] * w_pool_ref.shape[1]
    pg = w_pool_ref.shape[1]
    cw = w_conv_ref.shape[1]
    s = pl.program_id(1)
    t0 = s * tm

    shift = mod_ref[3:4, :]
    scale = mod_ref[4:5, :]
    xe = jnp.concatenate([prev_ref[...], x_ref[...], next_ref[...]], axis=0)
    h = (xe * (1.0 + scale) + shift).astype(BF16)

    pos_e = t0 - HALO + lax.broadcasted_iota(jnp.int32, (rows, 1), 0)
    valid = jnp.logical_and(pos_e >= 0, pos_e < seq)
    pos = t0 + lax.broadcasted_iota(jnp.int32, (tm, 1), 0)

    for g, win in enumerate(POOL_WINDOWS):
        half = win // 2
        u = jnp.dot(h, w_in_ref[:, g * pg:(g + 1) * pg], preferred_element_type=F32)
        slab_ref[...] = jnp.where(valid, u, 0.0)
        tot = slab_ref[pl.ds(HALO - half, tm), :]
        for k in range(1 - half, half):
            tot = tot + slab_ref[pl.ds(HALO + k, tm), :]
        cnt = (jnp.minimum(pos + half, seq) - jnp.maximum(pos - half, 0)).astype(F32)
        pooled = tot / cnt - slab_ref[pl.ds(HALO, tm), :]
        pm = jnp.dot(pooled.astype(BF16), w_pool_ref[g], preferred_element_type=F32)
        heads_ref[:, g * pg:(g + 1) * pg] = (pm * s_pool_ref[:, g * pg:(g + 1) * pg]).astype(BF16)

    for k in range(cw // pg):
        c0 = k * pg
        bg = jnp.dot(h, w_in_ref[:, pool_w + c0:pool_w + c0 + pg], preferred_element_type=F32)
        cg = jnp.dot(h, w_in_ref[:, pool_w + cw + c0:pool_w + cw + c0 + pg],
                     preferred_element_type=F32)
        v = jnp.dot(h, w_in_ref[:, pool_w + 2 * cw + c0:pool_w + 2 * cw + c0 + pg],
                    preferred_element_type=F32)
        slab_ref[...] = jnp.where(valid, cg * v, 0.0)
        y = (slab_ref[pl.ds(HALO - 1, tm), :] * w_conv_ref[0:1, c0:c0 + pg]
             + slab_ref[pl.ds(HALO, tm), :] * w_conv_ref[1:2, c0:c0 + pg]
             + slab_ref[pl.ds(HALO + 1, tm), :] * w_conv_ref[2:3, c0:c0 + pg])
        heads_ref[:, pool_w + c0:pool_w + c0 + pg] = (bg[HALO:HALO + tm, :] * y).astype(BF16)

    m = jnp.dot(heads_ref[...], w_out_ref[...], preferred_element_type=F32)
    gate = 1.0 + mod_ref[5:6, :]
    r = alpha * x_ref[...] + gate * m
    o_ref[...] = _layer_norm(r, lng_ref[...], lnb_ref[...])


def _mixer_sublayer(x, mod, w_in, w_pool, s_pool, w_conv, w_out, ln_g, ln_b, *, alpha):
    bsz, seq, d = x.shape
    tm = MIX_TOKENS
    hb = tm // HALO
    n_hblk = seq // HALO
    in_w = w_in.shape[1]
    mix_w = w_out.shape[0]
    pg = w_pool.shape[1]
    const = dict(pipeline_mode=pl.Buffered(1))
    kern = functools.partial(_mixer_kernel, seq=seq, alpha=alpha)
    return pl.pallas_call(
        kern,
        grid=(bsz, seq // tm),
        in_specs=[
            pl.BlockSpec((None, HALO, d), lambda b, s: (b, jnp.maximum(s * hb - 1, 0), 0)),
            pl.BlockSpec((None, tm, d), lambda b, s: (b, s, 0)),
            pl.BlockSpec((None, HALO, d),
                         lambda b, s: (b, jnp.minimum((s + 1) * hb, n_hblk - 1), 0)),
            pl.BlockSpec((None, N_MOD, d), lambda b, s: (b, 0, 0)),
            pl.BlockSpec((d, in_w), lambda b, s: (0, 0), **const),
            pl.BlockSpec(w_pool.shape, lambda b, s: (0, 0, 0), **const),
            pl.BlockSpec((1, s_pool.shape[1]), lambda b, s: (0, 0), **const),
            pl.BlockSpec(w_conv.shape, lambda b, s: (0, 0), **const),
            pl.BlockSpec((mix_w, d), lambda b, s: (0, 0), **const),
            pl.BlockSpec((1, d), lambda b, s: (0, 0), **const),
            pl.BlockSpec((1, d), lambda b, s: (0, 0), **const),
        ],
        out_specs=pl.BlockSpec((None, tm, d), lambda b, s: (b, s, 0)),
        out_shape=jax.ShapeDtypeStruct(x.shape, F32),
        scratch_shapes=[pltpu.VMEM((tm, mix_w), BF16),
                        pltpu.VMEM((tm + 2 * HALO, pg), F32)],
        compiler_params=pltpu.CompilerParams(
            dimension_semantics=("parallel", "arbitrary"),
            vmem_limit_bytes=VMEM_LIMIT),
        name="mixer",
    )(x, x, x, mod, w_in, w_pool, s_pool, w_conv, w_out, ln_g, ln_b)


def _pad_to(a, axis, mult):
    pad = (-a.shape[axis]) % mult
    if pad == 0:
        return a
    widths = [(0, 0)] * a.ndim
    widths[axis] = (0, pad)
    return jnp.pad(a, widths)


def kernel(x_prompt, x_sample, c_prompt, c_sample, w_ada, b_ada, ffn1_w1, ffn1_w3, ffn1_w2,
           w_in, w_pool, s_pool, w_conv, w_out, ffn2_w1, ffn2_w3, ffn2_w2, ln_g, ln_b):
    depth = w_ada.shape[0]
    d = x_prompt.shape[-1]
    alpha = (2.0 * depth) ** 0.25
    nb_p, nb_s = c_prompt.shape[0], c_sample.shape[0]

    c_all = _pad_to(jnp.concatenate([c_prompt, c_sample], axis=0), 0, V7X_F32_SUBLANES)
    xs = [x_prompt, x_sample]
    for l in range(depth):
        mod = _ada_ln(c_all, w_ada[l], b_ada[l])
        mods = [mod[:nb_p].reshape(nb_p, N_MOD, d), mod[nb_p:nb_p + nb_s].reshape(nb_s, N_MOD, d)]

        def ffn_weights(w1, w3, w2):
            return (_pad_to(w1.astype(BF16), 1, FFN_HIDDEN), _pad_to(w3.astype(BF16), 1, FFN_HIDDEN),
                    _pad_to(w2.astype(BF16), 0, FFN_HIDDEN))

        f1 = ffn_weights(ffn1_w1[l], ffn1_w3[l], ffn1_w2[l])
        f2 = ffn_weights(ffn2_w1[l], ffn2_w3[l], ffn2_w2[l])
        w_in_b, w_pool_b, w_out_b = w_in[l].astype(BF16), w_pool[l].astype(BF16), w_out[l].astype(BF16)
        s_pool_l = s_pool[l].reshape(1, -1)
        g = [ln_g[l, i].reshape(1, d) for i in range(3)]
        bb = [ln_b[l, i].reshape(1, d) for i in range(3)]

        for i in range(2):
            x = xs[i]
            x = _ffn_sublayer(x, mods[i], *f1, g[0], bb[0], mod_base=0, alpha=alpha)
            x = _mixer_sublayer(x, mods[i], w_in_b, w_pool_b, s_pool_l, w_conv[l], w_out_b,
                                g[1], bb[1], alpha=alpha)
            x = _ffn_sublayer(x, mods[i], *f2, g[2], bb[2], mod_base=6, alpha=alpha)
            xs[i] = x
    return (xs[0], xs[1])
```

```python
import functools

import jax
import jax.numpy as jnp
from jax import lax
from jax.experimental import pallas as pl
from jax.experimental.pallas import tpu as pltpu

F32 = jnp.float32
BF16 = jnp.bfloat16

POOL_WINDOWS = (2, 4, 8, 16)
CONV_K = 3
N_MOD = 9
LN_EPS = 1e-5

V7X_F32_SUBLANES = 8

FFN_TOKENS = 512
FFN_HIDDEN = 512
MIX_TOKENS = 512
MIX_COLS = 512
HALO = V7X_F32_SUBLANES
ADA_COLS = 1024
VMEM_LIMIT = 60 * 1024 * 1024

assert HALO >= max(POOL_WINDOWS) // 2 and HALO >= CONV_K // 2


def _layer_norm(r, g, b):
    mu = jnp.mean(r, axis=-1, keepdims=True)
    xc = r - mu
    var = jnp.mean(xc * xc, axis=-1, keepdims=True)
    return xc * lax.rsqrt(var + LN_EPS) * g + b


def _ada_kernel(c_ref, w_ref, b_ref, o_ref):
    c = c_ref[...]
    h = (c * jax.nn.sigmoid(c)).astype(BF16)
    o_ref[...] = jnp.dot(h, w_ref[...].astype(BF16), preferred_element_type=F32) + b_ref[...]


def _ada_ln(c, w_ada, b_ada):
    rows, d = c.shape
    n = w_ada.shape[1]
    return pl.pallas_call(
        _ada_kernel,
        grid=(n // ADA_COLS,),
        in_specs=[
            pl.BlockSpec((rows, d), lambda j: (0, 0)),
            pl.BlockSpec((d, ADA_COLS), lambda j: (0, j)),
            pl.BlockSpec((1, ADA_COLS), lambda j: (0, j)),
        ],
        out_specs=pl.BlockSpec((rows, ADA_COLS), lambda j: (0, j)),
        out_shape=jax.ShapeDtypeStruct((rows, n), F32),
        compiler_params=pltpu.CompilerParams(
            dimension_semantics=("arbitrary",), vmem_limit_bytes=VMEM_LIMIT),
        name="ada_ln",
    )(c, w_ada, b_ada.reshape(1, n))


def _ffn_kernel(x_ref, mod_ref, w1_ref, w3_ref, w2_ref, lng_ref, lnb_ref, o_ref,
                h_ref, acc_ref, *, mod_base, alpha, n_tiles, fin_rows):
    i = pl.program_id(0)
    j = pl.program_id(1)
    tm = x_ref.shape[0]
    slot = lax.rem(i, 2)

    @pl.when(jnp.logical_and(i == 0, j == 0))
    def _():
        acc_ref[1] = jnp.zeros(acc_ref.shape[1:], F32)

    @pl.when(jnp.logical_and(j == 0, i < n_tiles))
    def _():
        shift = mod_ref[mod_base:mod_base + 1, :]
        scale = mod_ref[mod_base + 1:mod_base + 2, :]
        x = x_ref[...]
        h_ref[...] = (x * (1.0 + scale) + shift).astype(BF16)
        acc_ref[slot] = alpha * x

    def finalize_rows():
        off = pl.multiple_of(jnp.minimum(j * fin_rows, tm - fin_rows), V7X_F32_SUBLANES)
        r = acc_ref[1 - slot, pl.ds(off, fin_rows), :]
        o_ref[pl.ds(off, fin_rows), :] = _layer_norm(r, lng_ref[...], lnb_ref[...])

    @pl.when(i < n_tiles)
    def _():
        finalize_rows()
        h = h_ref[...]
        a = jnp.dot(h, w1_ref[...], preferred_element_type=F32)
        b = jnp.dot(h, w3_ref[...], preferred_element_type=F32)
        hid = (a * jax.nn.sigmoid(a) * b).astype(BF16)
        p = jnp.dot(hid, w2_ref[...], preferred_element_type=F32)
        gate = 0.5 * (1.0 + mod_ref[mod_base + 2:mod_base + 3, :])
        acc_ref[slot] += gate * p

    @pl.when(i == n_tiles)
    def _():
        finalize_rows()


def _ffn_sublayer(x, mod, w1, w3, w2, ln_g, ln_b, *, mod_base, alpha):
    bsz, seq, d = x.shape
    fp = w1.shape[1]
    tm, tf = FFN_TOKENS, FFN_HIDDEN
    tiles_per_seq = seq // tm
    n_tiles = bsz * tiles_per_seq
    nj = fp // tf
    fin_rows = -(-tm // (nj * V7X_F32_SUBLANES)) * V7X_F32_SUBLANES
    assert fin_rows * nj >= tm and fin_rows <= tm

    def cur(i):
        return jnp.minimum(i, n_tiles - 1)

    def wcol(i, j):
        return jnp.where(i < n_tiles, j, nj - 1)

    kern = functools.partial(_ffn_kernel, mod_base=mod_base, alpha=alpha, n_tiles=n_tiles,
                             fin_rows=fin_rows)
    out = pl.pallas_call(
        kern,
        grid=(n_tiles + 1, nj),
        in_specs=[
            pl.BlockSpec((tm, d), lambda i, j: (cur(i), 0)),
            pl.BlockSpec((None, N_MOD, d), lambda i, j: (cur(i) // tiles_per_seq, 0, 0)),
            pl.BlockSpec((d, tf), lambda i, j: (0, wcol(i, j))),
            pl.BlockSpec((d, tf), lambda i, j: (0, wcol(i, j))),
            pl.BlockSpec((tf, d), lambda i, j: (wcol(i, j), 0)),
            pl.BlockSpec((1, d), lambda i, j: (0, 0)),
            pl.BlockSpec((1, d), lambda i, j: (0, 0)),
        ],
        out_specs=pl.BlockSpec((tm, d), lambda i, j: (jnp.maximum(i - 1, 0), 0)),
        out_shape=jax.ShapeDtypeStruct((bsz * seq, d), F32),
        scratch_shapes=[pltpu.VMEM((tm, d), BF16), pltpu.VMEM((2, tm, d), F32)],
        compiler_params=pltpu.CompilerParams(
            dimension_semantics=("arbitrary", "arbitrary"),
            vmem_limit_bytes=VMEM_LIMIT),
        name="ffn",
    )(x.reshape(bsz * seq, d), mod, w1, w3, w2, ln_g, ln_b)
    return out.reshape(bsz, seq, d)


def _mixer_kernel(prev_ref, x_ref, next_ref, mod_ref, w_in_ref, w_pool_ref, s_pool_ref,
                  w_conv_ref, w_out_ref, lng_ref, lnb_ref, o_ref,
                  r_ref, heads_ref, u_ref, z_ref, *, seq, alpha, n_tiles):
    tm = x_ref.shape[0]
    rows = tm + 2 * HALO
    n_groups, pg = w_pool_ref.shape[0], w_pool_ref.shape[1]
    pool_w = n_groups * pg
    cw = w_conv_ref.shape[1]
    i = pl.program_id(0)

    @pl.when(i == 0)
    def _():
        r_ref[...] = jnp.zeros(r_ref.shape, F32)

    def finalize():
        o_ref[...] = _layer_norm(r_ref[...], lng_ref[...], lnb_ref[...])

    @pl.when(i < n_tiles)
    def _():
        finalize()
        t0 = lax.rem(i, seq // tm) * tm
        shift = mod_ref[3:4, :]
        scale = mod_ref[4:5, :]
        x = x_ref[...]
        xe = jnp.concatenate([prev_ref[...], x, next_ref[...]], axis=0)
        h = (xe * (1.0 + scale) + shift).astype(BF16)

        pos_e = t0 - HALO + lax.broadcasted_iota(jnp.int32, (rows, 1), 0)
        valid = jnp.logical_and(pos_e >= 0, pos_e < seq)
        pos = t0 + lax.broadcasted_iota(jnp.int32, (tm, 1), 0)

        def proj(c0):
            return jnp.dot(h, w_in_ref[:, c0:c0 + MIX_COLS], preferred_element_type=F32)

        for c0 in range(0, pool_w, MIX_COLS):
            u_ref[:, c0:c0 + MIX_COLS] = jnp.where(valid, proj(c0), 0.0)
        for c0 in range(0, cw, MIX_COLS):
            cz = proj(pool_w + cw + c0) * proj(pool_w + 2 * cw + c0)
            z_ref[:, c0:c0 + MIX_COLS] = jnp.where(valid, cz, 0.0)

        for g, win in enumerate(POOL_WINDOWS):
            half = win // 2
            cols = slice(g * pg, (g + 1) * pg)
            tot = u_ref[pl.ds(HALO - half, tm), cols]
            for k in range(1 - half, half):
                tot = tot + u_ref[pl.ds(HALO + k, tm), cols]
            cnt = (jnp.minimum(pos + half, seq) - jnp.maximum(pos - half, 0)).astype(F32)
            pooled = tot / cnt - u_ref[pl.ds(HALO, tm), cols]
            pm = jnp.dot(pooled.astype(BF16), w_pool_ref[g], preferred_element_type=F32)
            heads_ref[:, cols] = (pm * s_pool_ref[:, cols]).astype(BF16)

        for c0 in range(0, cw, MIX_COLS):
            cols = slice(c0, c0 + MIX_COLS)
            bg = proj(pool_w + c0)
            y = (z_ref[pl.ds(HALO - 1, tm), cols] * w_conv_ref[0:1, cols]
                 + z_ref[pl.ds(HALO, tm), cols] * w_conv_ref[1:2, cols]
                 + z_ref[pl.ds(HALO + 1, tm), cols] * w_conv_ref[2:3, cols])
            heads_ref[:, pool_w + c0:pool_w + c0 + MIX_COLS] = (bg[HALO:HALO + tm, :] * y).astype(BF16)

        m = jnp.dot(heads_ref[...], w_out_ref[...], preferred_element_type=F32)
        gate = 1.0 + mod_ref[5:6, :]
        r_ref[...] = alpha * x + gate * m

    @pl.when(i == n_tiles)
    def _():
        finalize()


def _mixer_sublayer(x, mod, w_in, w_pool, s_pool, w_conv, w_out, ln_g, ln_b, *, alpha):
    bsz, seq, d = x.shape
    tm = MIX_TOKENS
    tiles_per_seq = seq // tm
    n_tiles = bsz * tiles_per_seq
    hb = tm // HALO
    n_hblk = bsz * seq // HALO
    in_w = w_in.shape[1]
    mix_w = w_out.shape[0]
    pool_w = w_pool.shape[0] * w_pool.shape[1]
    cw = w_conv.shape[1]
    const = dict(pipeline_mode=pl.Buffered(1))

    def cur(i):
        return jnp.minimum(i, n_tiles - 1)

    kern = functools.partial(_mixer_kernel, seq=seq, alpha=alpha, n_tiles=n_tiles)
    x2 = x.reshape(bsz * seq, d)
    out = pl.pallas_call(
        kern,
        grid=(n_tiles + 1,),
        in_specs=[
            pl.BlockSpec((HALO, d), lambda i: (jnp.maximum(cur(i) * hb - 1, 0), 0)),
            pl.BlockSpec((tm, d), lambda i: (cur(i), 0)),
            pl.BlockSpec((HALO, d), lambda i: (jnp.minimum((cur(i) + 1) * hb, n_hblk - 1), 0)),
            pl.BlockSpec((None, N_MOD, d), lambda i: (cur(i) // tiles_per_seq, 0, 0)),
            pl.BlockSpec((d, in_w), lambda i: (0, 0), **const),
            pl.BlockSpec(w_pool.shape, lambda i: (0, 0, 0), **const),
            pl.BlockSpec((1, pool_w), lambda i: (0, 0), **const),
            pl.BlockSpec(w_conv.shape, lambda i: (0, 0), **const),
            pl.BlockSpec((mix_w, d), lambda i: (0, 0), **const),
            pl.BlockSpec((1, d), lambda i: (0, 0), **const),
            pl.BlockSpec((1, d), lambda i: (0, 0), **const),
        ],
        out_specs=pl.BlockSpec((tm, d), lambda i: (jnp.maximum(i - 1, 0), 0)),
        out_shape=jax.ShapeDtypeStruct((bsz * seq, d), F32),
        scratch_shapes=[pltpu.VMEM((tm, d), F32),
                        pltpu.VMEM((tm, mix_w), BF16),
                        pltpu.VMEM((tm + 2 * HALO, pool_w), F32),
                        pltpu.VMEM((tm + 2 * HALO, cw), F32)],
        compiler_params=pltpu.CompilerParams(
            dimension_semantics=("arbitrary",),
            vmem_limit_bytes=VMEM_LIMIT),
        name="mixer",
    )(x2, x2, x2, mod, w_in, w_pool, s_pool, w_conv, w_out, ln_g, ln_b)
    return out.reshape(bsz, seq, d)


def _pad_to(a, axis, mult):
    pad = (-a.shape[axis]) % mult
    if pad == 0:
        return a
    widths = [(0, 0)] * a.ndim
    widths[axis] = (0, pad)
    return jnp.pad(a, widths)


def kernel(x_prompt, x_sample, c_prompt, c_sample, w_ada, b_ada, ffn1_w1, ffn1_w3, ffn1_w2,
           w_in, w_pool, s_pool, w_conv, w_out, ffn2_w1, ffn2_w3, ffn2_w2, ln_g, ln_b):
    depth = w_ada.shape[0]
    d = x_prompt.shape[-1]
    alpha = (2.0 * depth) ** 0.25
    nb_p, nb_s = c_prompt.shape[0], c_sample.shape[0]

    c_all = _pad_to(jnp.concatenate([c_prompt, c_sample], axis=0), 0, V7X_F32_SUBLANES)
    xs = [x_prompt, x_sample]
    for l in range(depth):
        mod = _ada_ln(c_all, w_ada[l], b_ada[l])
        mods = [mod[:nb_p].reshape(nb_p, N_MOD, d), mod[nb_p:nb_p + nb_s].reshape(nb_s, N_MOD, d)]

        def ffn_weights(w1, w3, w2):
            return (_pad_to(w1.astype(BF16), 1, FFN_HIDDEN), _pad_to(w3.astype(BF16), 1, FFN_HIDDEN),
                    _pad_to(w2.astype(BF16), 0, FFN_HIDDEN))

        f1 = ffn_weights(ffn1_w1[l], ffn1_w3[l], ffn1_w2[l])
        f2 = ffn_weights(ffn2_w1[l], ffn2_w3[l], ffn2_w2[l])
        w_in_b, w_pool_b, w_out_b = w_in[l].astype(BF16), w_pool[l].astype(BF16), w_out[l].astype(BF16)
        s_pool_l = s_pool[l].reshape(1, -1)
        g = [ln_g[l, i].reshape(1, d) for i in range(3)]
        bb = [ln_b[l, i].reshape(1, d) for i in range(3)]

        for i in range(2):
            x = xs[i]
            x = _ffn_sublayer(x, mods[i], *f1, g[0], bb[0], mod_base=0, alpha=alpha)
            x = _mixer_sublayer(x, mods[i], w_in_b, w_pool_b, s_pool_l, w_conv[l], w_out_b,
                                g[1], bb[1], alpha=alpha)
            x = _ffn_sublayer(x, mods[i], *f2, g[2], bb[2], mod_base=6, alpha=alpha)
            xs[i] = x
    return (xs[0], xs[1])
```

```python
import functools

import jax
import jax.numpy as jnp
from jax import lax
from jax.experimental import pallas as pl
from jax.experimental.pallas import tpu as pltpu

F32 = jnp.float32
BF16 = jnp.bfloat16

POOL_WINDOWS = (2, 4, 8, 16)
CONV_K = 3
N_MOD = 9
LN_EPS = 1e-5

V7X_F32_SUBLANES = 8

FFN_TOKENS = 512
FFN_HIDDEN = 512
MIX_TOKENS = 512
MIX_COLS = 512
HALO = V7X_F32_SUBLANES
ADA_COLS = 1024
VMEM_LIMIT = 60 * 1024 * 1024

assert HALO >= max(POOL_WINDOWS) // 2 and HALO >= CONV_K // 2


def _layer_norm(r, g, b):
    mu = jnp.mean(r, axis=-1, keepdims=True)
    xc = r - mu
    var = jnp.mean(xc * xc, axis=-1, keepdims=True)
    return xc * lax.rsqrt(var + LN_EPS) * g + b


def _ada_kernel(c_ref, w_ref, b_ref, o_ref):
    c = c_ref[...]
    h = (c * jax.nn.sigmoid(c)).astype(BF16)
    o_ref[...] = jnp.dot(h, w_ref[...].astype(BF16), preferred_element_type=F32) + b_ref[...]


def _ada_ln(c, w_ada, b_ada):
    rows, d = c.shape
    n = w_ada.shape[1]
    return pl.pallas_call(
        _ada_kernel,
        grid=(n // ADA_COLS,),
        in_specs=[
            pl.BlockSpec((rows, d), lambda j: (0, 0)),
            pl.BlockSpec((d, ADA_COLS), lambda j: (0, j)),
            pl.BlockSpec((1, ADA_COLS), lambda j: (0, j)),
        ],
        out_specs=pl.BlockSpec((rows, ADA_COLS), lambda j: (0, j)),
        out_shape=jax.ShapeDtypeStruct((rows, n), F32),
        compiler_params=pltpu.CompilerParams(
            dimension_semantics=("arbitrary",), vmem_limit_bytes=VMEM_LIMIT),
        name="ada_ln",
    )(c, w_ada, b_ada.reshape(1, n))


def _ffn_kernel(x_ref, mod_ref, w1_ref, w3_ref, w2_ref, lng_ref, lnb_ref, o_ref,
                h_ref, acc_ref, *, mod_base, alpha, n_tiles, fin_rows):
    i = pl.program_id(0)
    j = pl.program_id(1)
    tm = x_ref.shape[0]
    slot = lax.rem(i, 2)

    @pl.when(jnp.logical_and(i == 0, j == 0))
    def _():
        acc_ref[1] = jnp.zeros(acc_ref.shape[1:], F32)

    @pl.when(jnp.logical_and(j == 0, i < n_tiles))
    def _():
        shift = mod_ref[mod_base:mod_base + 1, :]
        scale = mod_ref[mod_base + 1:mod_base + 2, :]
        x = x_ref[...]
        h_ref[...] = (x * (1.0 + scale) + shift).astype(BF16)
        acc_ref[slot] = alpha * x

    def finalize_rows():
        off = pl.multiple_of(jnp.minimum(j * fin_rows, tm - fin_rows), V7X_F32_SUBLANES)
        r = acc_ref[1 - slot, pl.ds(off, fin_rows), :]
        o_ref[pl.ds(off, fin_rows), :] = _layer_norm(r, lng_ref[...], lnb_ref[...])

    @pl.when(i < n_tiles)
    def _():
        finalize_rows()
        h = h_ref[...]
        a = jnp.dot(h, w1_ref[...], preferred_element_type=F32)
        b = jnp.dot(h, w3_ref[...], preferred_element_type=F32)
        hid = (a * jax.nn.sigmoid(a) * b).astype(BF16)
        p = jnp.dot(hid, w2_ref[...], preferred_element_type=F32)
        gate = 0.5 * (1.0 + mod_ref[mod_base + 2:mod_base + 3, :])
        acc_ref[slot] += gate * p

    @pl.when(i == n_tiles)
    def _():
        finalize_rows()


def _ffn_sublayer(x, mod, w1, w3, w2, ln_g, ln_b, *, mod_base, alpha):
    bsz, seq, d = x.shape
    fp = w1.shape[1]
    tm, tf = FFN_TOKENS, FFN_HIDDEN
    tiles_per_seq = seq // tm
    n_tiles = bsz * tiles_per_seq
    nj = fp // tf
    fin_rows = -(-tm // (nj * V7X_F32_SUBLANES)) * V7X_F32_SUBLANES
    assert fin_rows * nj >= tm and fin_rows <= tm

    def cur(i):
        return jnp.minimum(i, n_tiles - 1)

    def wcol(i, j):
        return jnp.where(i < n_tiles, j, nj - 1)

    kern = functools.partial(_ffn_kernel, mod_base=mod_base, alpha=alpha, n_tiles=n_tiles,
                             fin_rows=fin_rows)
    out = pl.pallas_call(
        kern,
        grid=(n_tiles + 1, nj),
        in_specs=[
            pl.BlockSpec((tm, d), lambda i, j: (cur(i), 0)),
            pl.BlockSpec((None, N_MOD, d), lambda i, j: (cur(i) // tiles_per_seq, 0, 0)),
            pl.BlockSpec((d, tf), lambda i, j: (0, wcol(i, j))),
            pl.BlockSpec((d, tf), lambda i, j: (0, wcol(i, j))),
            pl.BlockSpec((tf, d), lambda i, j: (wcol(i, j), 0)),
            pl.BlockSpec((1, d), lambda i, j: (0, 0)),
            pl.BlockSpec((1, d), lambda i, j: (0, 0)),
        ],
        out_specs=pl.BlockSpec((tm, d), lambda i, j: (jnp.maximum(i - 1, 0), 0)),
        out_shape=jax.ShapeDtypeStruct((bsz * seq, d), F32),
        scratch_shapes=[pltpu.VMEM((tm, d), BF16), pltpu.VMEM((2, tm, d), F32)],
        compiler_params=pltpu.CompilerParams(
            dimension_semantics=("arbitrary", "arbitrary"),
            vmem_limit_bytes=VMEM_LIMIT),
        name="ffn",
    )(x.reshape(bsz * seq, d), mod, w1, w3, w2, ln_g, ln_b)
    return out.reshape(bsz, seq, d)


def _mixer_kernel(prev_ref, x_ref, next_ref, mod_ref, w_in_ref, w_pool_ref, s_pool_ref,
                  w_conv_ref, w_out_ref, lng_ref, lnb_ref, o_ref,
                  r_ref, heads_ref, u_ref, z_ref, *, seq, alpha, n_tiles):
    tm = x_ref.shape[0]
    rows = tm + 2 * HALO
    n_groups, pg = w_pool_ref.shape[0], w_pool_ref.shape[1]
    pool_w = n_groups * pg
    cw = w_conv_ref.shape[1]
    i = pl.program_id(0)

    @pl.when(i == 0)
    def _():
        r_ref[...] = jnp.zeros(r_ref.shape, F32)

    def finalize():
        o_ref[...] = _layer_norm(r_ref[...], lng_ref[...], lnb_ref[...])

    @pl.when(i < n_tiles)
    def _():
        finalize()
        t0 = lax.rem(i, seq // tm) * tm
        shift = mod_ref[3:4, :]
        scale = mod_ref[4:5, :]
        x = x_ref[...]
        xe = jnp.concatenate([prev_ref[...], x, next_ref[...]], axis=0)
        h = (xe * (1.0 + scale) + shift).astype(BF16)

        pos_e = t0 - HALO + lax.broadcasted_iota(jnp.int32, (rows, 1), 0)
        valid = jnp.logical_and(pos_e >= 0, pos_e < seq)
        pos = t0 + lax.broadcasted_iota(jnp.int32, (tm, 1), 0)

        def proj(c0):
            return jnp.dot(h, w_in_ref[:, c0:c0 + MIX_COLS], preferred_element_type=F32)

        for c0 in range(0, pool_w, MIX_COLS):
            u_ref[:, c0:c0 + MIX_COLS] = jnp.where(valid, proj(c0), 0.0)
        for c0 in range(0, cw, MIX_COLS):
            cz = proj(pool_w + cw + c0) * proj(pool_w + 2 * cw + c0)
            z_ref[:, c0:c0 + MIX_COLS] = jnp.where(valid, cz, 0.0)

        for g, win in enumerate(POOL_WINDOWS):
            half = win // 2
            cols = slice(g * pg, (g + 1) * pg)
            tot = u_ref[pl.ds(HALO - half, tm), cols]
            for k in range(1 - half, half):
                tot = tot + u_ref[pl.ds(HALO + k, tm), cols]
            cnt = (jnp.minimum(pos + half, seq) - jnp.maximum(pos - half, 0)).astype(F32)
            pooled = tot / cnt - u_ref[pl.ds(HALO, tm), cols]
            pm = jnp.dot(pooled.astype(BF16), w_pool_ref[g], preferred_element_type=F32)
            heads_ref[:, cols] = (pm * s_pool_ref[:, cols]).astype(BF16)

        for c0 in range(0, cw, MIX_COLS):
            cols = slice(c0, c0 + MIX_COLS)
            bg = proj(pool_w + c0)
            y = (z_ref[pl.ds(HALO - 1, tm), cols] * w_conv_ref[0:1, cols]
                 + z_ref[pl.ds(HALO, tm), cols] * w_conv_ref[1:2, cols]
                 + z_ref[pl.ds(HALO + 1, tm), cols] * w_conv_ref[2:3, cols])
            heads_ref[:, pool_w + c0:pool_w + c0 + MIX_COLS] = (bg[HALO:HALO + tm, :] * y).astype(BF16)

        m = jnp.dot(heads_ref[...], w_out_ref[...], preferred_element_type=F32)
        gate = 1.0 + mod_ref[5:6, :]
        r_ref[...] = alpha * x + gate * m

    @pl.when(i == n_tiles)
    def _():
        finalize()


def _mixer_sublayer(x, mod, w_in, w_pool, s_pool, w_conv, w_out, ln_g, ln_b, *, alpha):
    bsz, seq, d = x.shape
    tm = MIX_TOKENS
    tiles_per_seq = seq // tm
    n_tiles = bsz * tiles_per_seq
    hb = tm // HALO
    n_hblk = bsz * seq // HALO
    in_w = w_in.shape[1]
    mix_w = w_out.shape[0]
    pool_w = w_pool.shape[0] * w_pool.shape[1]
    cw = w_conv.shape[1]
    const = dict(pipeline_mode=pl.Buffered(1))

    def cur(i):
        return jnp.minimum(i, n_tiles - 1)

    kern = functools.partial(_mixer_kernel, seq=seq, alpha=alpha, n_tiles=n_tiles)
    x2 = x.reshape(bsz * seq, d)
    out = pl.pallas_call(
        kern,
        grid=(n_tiles + 1,),
        in_specs=[
            pl.BlockSpec((HALO, d), lambda i: (jnp.maximum(cur(i) * hb - 1, 0), 0)),
            pl.BlockSpec((tm, d), lambda i: (cur(i), 0)),
            pl.BlockSpec((HALO, d), lambda i: (jnp.minimum((cur(i) + 1) * hb, n_hblk - 1), 0)),
            pl.BlockSpec((None, N_MOD, d), lambda i: (cur(i) // tiles_per_seq, 0, 0)),
            pl.BlockSpec((d, in_w), lambda i: (0, 0), **const),
            pl.BlockSpec(w_pool.shape, lambda i: (0, 0, 0), **const),
            pl.BlockSpec((1, pool_w), lambda i: (0, 0), **const),
            pl.BlockSpec(w_conv.shape, lambda i: (0, 0), **const),
            pl.BlockSpec((mix_w, d), lambda i: (0, 0), **const),
            pl.BlockSpec((1, d), lambda i: (0, 0), **const),
            pl.BlockSpec((1, d), lambda i: (0, 0), **const),
        ],
        out_specs=pl.BlockSpec((tm, d), lambda i: (jnp.maximum(i - 1, 0), 0)),
        out_shape=jax.ShapeDtypeStruct((bsz * seq, d), F32),
        scratch_shapes=[pltpu.VMEM((tm, d), F32),
                        pltpu.VMEM((tm, mix_w), BF16),
                        pltpu.VMEM((tm + 2 * HALO, pool_w), F32),
                        pltpu.VMEM((tm + 2 * HALO, cw), F32)],
        compiler_params=pltpu.CompilerParams(
            dimension_semantics=("arbitrary",),
            vmem_limit_bytes=VMEM_LIMIT),
        name="mixer",
    )(x2, x2, x2, mod, w_in, w_pool, s_pool, w_conv, w_out, ln_g, ln_b)
    return out.reshape(bsz, seq, d)


def _pad_to(a, axis, mult):
    pad = (-a.shape[axis]) % mult
    if pad == 0:
        return a
    widths = [(0, 0)] * a.ndim
    widths[axis] = (0, pad)
    return jnp.pad(a, widths)


def kernel(x_prompt, x_sample, c_prompt, c_sample, w_ada, b_ada, ffn1_w1, ffn1_w3, ffn1_w2,
           w_in, w_pool, s_pool, w_conv, w_out, ffn2_w1, ffn2_w3, ffn2_w2, ln_g, ln_b):
    depth = w_ada.shape[0]
    d = x_prompt.shape[-1]
    alpha = (2.0 * depth) ** 0.25
    nb_p, nb_s = c_prompt.shape[0], c_sample.shape[0]

    c_all = _pad_to(jnp.concatenate([c_prompt, c_sample], axis=0), 0, V7X_F32_SUBLANES)
    xs = [x_prompt, x_sample]
    for l in range(depth):
        mod = _ada_ln(c_all, w_ada[l], b_ada[l])
        mods = [mod[:nb_p].reshape(nb_p, N_MOD, d), mod[nb_p:nb_p + nb_s].reshape(nb_s, N_MOD, d)]

        def ffn_weights(w1, w3, w2):
            return (_pad_to(w1, 1, FFN_HIDDEN).astype(BF16), _pad_to(w3, 1, FFN_HIDDEN).astype(BF16),
                    _pad_to(w2, 0, FFN_HIDDEN).astype(BF16))

        f1 = ffn_weights(ffn1_w1[l], ffn1_w3[l], ffn1_w2[l])
        f2 = ffn_weights(ffn2_w1[l], ffn2_w3[l], ffn2_w2[l])
        w_in_b, w_pool_b, w_out_b = w_in[l].astype(BF16), w_pool[l].astype(BF16), w_out[l].astype(BF16)
        s_pool_l = s_pool[l].reshape(1, -1)
        g = [ln_g[l, i].reshape(1, d) for i in range(3)]
        bb = [ln_b[l, i].reshape(1, d) for i in range(3)]

        for i in range(2):
            x = xs[i]
            x = _ffn_sublayer(x, mods[i], *f1, g[0], bb[0], mod_base=0, alpha=alpha)
            x = _mixer_sublayer(x, mods[i], w_in_b, w_pool_b, s_pool_l, w_conv[l], w_out_b,
                                g[1], bb[1], alpha=alpha)
            x = _ffn_sublayer(x, mods[i], *f2, g[2], bb[2], mod_base=6, alpha=alpha)
            xs[i] = x
    return (xs[0], xs[1])
```

```python
import functools

import jax
import jax.numpy as jnp
from jax import lax
from jax.experimental import pallas as pl
from jax.experimental.pallas import tpu as pltpu

F32 = jnp.float32
BF16 = jnp.bfloat16

POOL_WINDOWS = (2, 4, 8, 16)
CONV_K = 3
N_MOD = 9
LN_EPS = 1e-5

V7X_F32_SUBLANES = 8

FFN_TOKENS = 512
FFN_HIDDEN = 512
MIX_TOKENS = 512
MIX_COLS = 512
HALO = V7X_F32_SUBLANES
ADA_COLS = 1024
VMEM_LIMIT = 60 * 1024 * 1024

assert HALO >= max(POOL_WINDOWS) // 2 and HALO >= CONV_K // 2


def _layer_norm(r, g, b):
    mu = jnp.mean(r, axis=-1, keepdims=True)
    xc = r - mu
    var = jnp.mean(xc * xc, axis=-1, keepdims=True)
    return xc * lax.rsqrt(var + LN_EPS) * g + b


def _ada_kernel(c_ref, w_ref, b_ref, o_ref):
    c = c_ref[...]
    h = (c * jax.nn.sigmoid(c)).astype(BF16)
    o_ref[...] = jnp.dot(h, w_ref[...].astype(BF16), preferred_element_type=F32) + b_ref[...]


def _ada_ln(c, w_ada, b_ada):
    rows, d = c.shape
    n = w_ada.shape[1]
    return pl.pallas_call(
        _ada_kernel,
        grid=(n // ADA_COLS,),
        in_specs=[
            pl.BlockSpec((rows, d), lambda j: (0, 0)),
            pl.BlockSpec((d, ADA_COLS), lambda j: (0, j)),
            pl.BlockSpec((1, ADA_COLS), lambda j: (0, j)),
        ],
        out_specs=pl.BlockSpec((rows, ADA_COLS), lambda j: (0, j)),
        out_shape=jax.ShapeDtypeStruct((rows, n), F32),
        compiler_params=pltpu.CompilerParams(
            dimension_semantics=("arbitrary",), vmem_limit_bytes=VMEM_LIMIT),
        name="ada_ln",
    )(c, w_ada, b_ada.reshape(1, n))


def _ffn_kernel(x_ref, mod_ref, w1_ref, w3_ref, w2_ref, lng_ref, lnb_ref, o_ref,
                h_ref, acc_ref, *, mod_base, alpha, n_tiles, fin_rows):
    i = pl.program_id(0)
    j = pl.program_id(1)
    tm = x_ref.shape[0]
    slot = lax.rem(i, 2)

    @pl.when(jnp.logical_and(i == 0, j == 0))
    def _():
        acc_ref[1] = jnp.zeros(acc_ref.shape[1:], F32)

    @pl.when(jnp.logical_and(j == 0, i < n_tiles))
    def _():
        shift = mod_ref[mod_base:mod_base + 1, :]
        scale = mod_ref[mod_base + 1:mod_base + 2, :]
        x = x_ref[...]
        h_ref[...] = (x * (1.0 + scale) + shift).astype(BF16)
        acc_ref[slot] = alpha * x

    def finalize_rows():
        off = pl.multiple_of(jnp.minimum(j * fin_rows, tm - fin_rows), V7X_F32_SUBLANES)
        r = acc_ref[1 - slot, pl.ds(off, fin_rows), :]
        o_ref[pl.ds(off, fin_rows), :] = _layer_norm(r, lng_ref[...], lnb_ref[...])

    @pl.when(i < n_tiles)
    def _():
        finalize_rows()
        h = h_ref[...]
        a = jnp.dot(h, w1_ref[...], preferred_element_type=F32)
        b = jnp.dot(h, w3_ref[...], preferred_element_type=F32)
        hid = (a * jax.nn.sigmoid(a) * b).astype(BF16)
        p = jnp.dot(hid, w2_ref[...], preferred_element_type=F32)
        gate = 0.5 * (1.0 + mod_ref[mod_base + 2:mod_base + 3, :])
        acc_ref[slot] += gate * p

    @pl.when(i == n_tiles)
    def _():
        finalize_rows()


def _ffn_sublayer(x, mod, w1, w3, w2, ln_g, ln_b, *, mod_base, alpha):
    bsz, seq, d = x.shape
    fp = w1.shape[1]
    tm, tf = FFN_TOKENS, FFN_HIDDEN
    tiles_per_seq = seq // tm
    n_tiles = bsz * tiles_per_seq
    nj = fp // tf
    fin_rows = -(-tm // (nj * V7X_F32_SUBLANES)) * V7X_F32_SUBLANES
    assert fin_rows * nj >= tm and fin_rows <= tm

    def cur(i):
        return jnp.minimum(i, n_tiles - 1)

    def wcol(i, j):
        return jnp.where(i < n_tiles, j, nj - 1)

    kern = functools.partial(_ffn_kernel, mod_base=mod_base, alpha=alpha, n_tiles=n_tiles,
                             fin_rows=fin_rows)
    out = pl.pallas_call(
        kern,
        grid=(n_tiles + 1, nj),
        in_specs=[
            pl.BlockSpec((tm, d), lambda i, j: (cur(i), 0)),
            pl.BlockSpec((None, N_MOD, d), lambda i, j: (cur(i) // tiles_per_seq, 0, 0)),
            pl.BlockSpec((d, tf), lambda i, j: (0, wcol(i, j))),
            pl.BlockSpec((d, tf), lambda i, j: (0, wcol(i, j))),
            pl.BlockSpec((tf, d), lambda i, j: (wcol(i, j), 0)),
            pl.BlockSpec((1, d), lambda i, j: (0, 0)),
            pl.BlockSpec((1, d), lambda i, j: (0, 0)),
        ],
        out_specs=pl.BlockSpec((tm, d), lambda i, j: (jnp.maximum(i - 1, 0), 0)),
        out_shape=jax.ShapeDtypeStruct((bsz * seq, d), F32),
        scratch_shapes=[pltpu.VMEM((tm, d), BF16), pltpu.VMEM((2, tm, d), F32)],
        compiler_params=pltpu.CompilerParams(
            dimension_semantics=("arbitrary", "arbitrary"),
            vmem_limit_bytes=VMEM_LIMIT),
        name="ffn",
    )(x.reshape(bsz * seq, d), mod, w1, w3, w2, ln_g, ln_b)
    return out.reshape(bsz, seq, d)


def _mixer_kernel(prev_ref, x_ref, next_ref, mod_ref, w_in_ref, w_pool_ref, s_pool_ref,
                  w_conv_ref, w_out_ref, lng_ref, lnb_ref, o_ref,
                  r_ref, heads_ref, u_ref, z_ref, *, seq, alpha, n_tiles):
    tm = x_ref.shape[0]
    rows = tm + 2 * HALO
    n_groups, pg = w_pool_ref.shape[0], w_pool_ref.shape[1]
    pool_w = n_groups * pg
    cw = w_conv_ref.shape[1]
    i = pl.program_id(0)

    @pl.when(i == 0)
    def _():
        r_ref[...] = jnp.zeros(r_ref.shape, F32)

    def finalize():
        o_ref[...] = _layer_norm(r_ref[...], lng_ref[...], lnb_ref[...])

    def finalize_rows_zero(r0, nr):
        y = _layer_norm(r_ref[r0:r0 + nr, :], lng_ref[...], lnb_ref[...])
        o_ref[r0:r0 + nr, :] = y
        bits = pltpu.bitcast(y[:, 0:MIX_COLS], jnp.uint32)
        acc = bits[0:V7X_F32_SUBLANES]
        for q in range(V7X_F32_SUBLANES, nr, V7X_F32_SUBLANES):
            acc = acc | bits[q:q + V7X_F32_SUBLANES]
        sixteen = jnp.uint32(16)
        zero = lax.shift_right_logical(lax.shift_right_logical(acc, sixteen), sixteen)
        return pltpu.bitcast(zero, F32)

    @pl.when(i < n_tiles)
    def _():
        n_anchor = 2 * (pool_w // MIX_COLS)
        nr = tm // n_anchor
        anchors = [jnp.concatenate([finalize_rows_zero(q * nr, nr)] * (rows // V7X_F32_SUBLANES), axis=0)
                   for q in range(n_anchor)]
        t0 = lax.rem(i, seq // tm) * tm
        shift = mod_ref[3:4, :]
        scale = mod_ref[4:5, :]
        x = x_ref[...]
        xe = jnp.concatenate([prev_ref[...], x, next_ref[...]], axis=0)
        h = (xe * (1.0 + scale) + shift).astype(BF16)

        pos_e = t0 - HALO + lax.broadcasted_iota(jnp.int32, (rows, 1), 0)
        valid = jnp.logical_and(pos_e >= 0, pos_e < seq)
        pos = t0 + lax.broadcasted_iota(jnp.int32, (tm, 1), 0)

        def proj(c0):
            return jnp.dot(h, w_in_ref[:, c0:c0 + MIX_COLS], preferred_element_type=F32)

        n_pool_stage = pool_w // MIX_COLS
        for c0 in range(0, pool_w, MIX_COLS):
            u_ref[:, c0:c0 + MIX_COLS] = jnp.where(valid, proj(c0), anchors[c0 // MIX_COLS])
        for c0 in range(0, cw, MIX_COLS):
            cz = proj(pool_w + cw + c0) * proj(pool_w + 2 * cw + c0)
            anchor = anchors[(n_pool_stage + c0 // MIX_COLS) % n_anchor]
            z_ref[:, c0:c0 + MIX_COLS] = jnp.where(valid, cz, anchor)

        for g, win in enumerate(POOL_WINDOWS):
            half = win // 2
            cols = slice(g * pg, (g + 1) * pg)
            tot = u_ref[pl.ds(HALO - half, tm), cols]
            for k in range(1 - half, half):
                tot = tot + u_ref[pl.ds(HALO + k, tm), cols]
            cnt = (jnp.minimum(pos + half, seq) - jnp.maximum(pos - half, 0)).astype(F32)
            pooled = tot / cnt - u_ref[pl.ds(HALO, tm), cols]
            pm = jnp.dot(pooled.astype(BF16), w_pool_ref[g], preferred_element_type=F32)
            heads_ref[:, cols] = (pm * s_pool_ref[:, cols]).astype(BF16)

        for c0 in range(0, cw, MIX_COLS):
            cols = slice(c0, c0 + MIX_COLS)
            bg = proj(pool_w + c0)
            y = (z_ref[pl.ds(HALO - 1, tm), cols] * w_conv_ref[0:1, cols]
                 + z_ref[pl.ds(HALO, tm), cols] * w_conv_ref[1:2, cols]
                 + z_ref[pl.ds(HALO + 1, tm), cols] * w_conv_ref[2:3, cols])
            heads_ref[:, pool_w + c0:pool_w + c0 + MIX_COLS] = (bg[HALO:HALO + tm, :] * y).astype(BF16)

        m = jnp.dot(heads_ref[...], w_out_ref[...], preferred_element_type=F32)
        gate = 1.0 + mod_ref[5:6, :]
        r_ref[...] = alpha * x + gate * m

    @pl.when(i == n_tiles)
    def _():
        finalize()


def _mixer_sublayer(x, mod, w_in, w_pool, s_pool, w_conv, w_out, ln_g, ln_b, *, alpha):
    bsz, seq, d = x.shape
    tm = MIX_TOKENS
    tiles_per_seq = seq // tm
    n_tiles = bsz * tiles_per_seq
    hb = tm // HALO
    n_hblk = bsz * seq // HALO
    in_w = w_in.shape[1]
    mix_w = w_out.shape[0]
    pool_w = w_pool.shape[0] * w_pool.shape[1]
    cw = w_conv.shape[1]
    const = dict(pipeline_mode=pl.Buffered(1))

    def cur(i):
        return jnp.minimum(i, n_tiles - 1)

    kern = functools.partial(_mixer_kernel, seq=seq, alpha=alpha, n_tiles=n_tiles)
    x2 = x.reshape(bsz * seq, d)
    out = pl.pallas_call(
        kern,
        grid=(n_tiles + 1,),
        in_specs=[
            pl.BlockSpec((HALO, d), lambda i: (jnp.maximum(cur(i) * hb - 1, 0), 0)),
            pl.BlockSpec((tm, d), lambda i: (cur(i), 0)),
            pl.BlockSpec((HALO, d), lambda i: (jnp.minimum((cur(i) + 1) * hb, n_hblk - 1), 0)),
            pl.BlockSpec((None, N_MOD, d), lambda i: (cur(i) // tiles_per_seq, 0, 0)),
            pl.BlockSpec((d, in_w), lambda i: (0, 0), **const),
            pl.BlockSpec(w_pool.shape, lambda i: (0, 0, 0), **const),
            pl.BlockSpec((1, pool_w), lambda i: (0, 0), **const),
            pl.BlockSpec(w_conv.shape, lambda i: (0, 0), **const),
            pl.BlockSpec((mix_w, d), lambda i: (0, 0), **const),
            pl.BlockSpec((1, d), lambda i: (0, 0), **const),
            pl.BlockSpec((1, d), lambda i: (0, 0), **const),
        ],
        out_specs=pl.BlockSpec((tm, d), lambda i: (jnp.maximum(i - 1, 0), 0)),
        out_shape=jax.ShapeDtypeStruct((bsz * seq, d), F32),
        scratch_shapes=[pltpu.VMEM((tm, d), F32),
                        pltpu.VMEM((tm, mix_w), BF16),
                        pltpu.VMEM((tm + 2 * HALO, pool_w), F32),
                        pltpu.VMEM((tm + 2 * HALO, cw), F32)],
        compiler_params=pltpu.CompilerParams(
            dimension_semantics=("arbitrary",),
            vmem_limit_bytes=VMEM_LIMIT),
        name="mixer",
    )(x2, x2, x2, mod, w_in, w_pool, s_pool, w_conv, w_out, ln_g, ln_b)
    return out.reshape(bsz, seq, d)


def _pad_to(a, axis, mult):
    pad = (-a.shape[axis]) % mult
    if pad == 0:
        return a
    widths = [(0, 0)] * a.ndim
    widths[axis] = (0, pad)
    return jnp.pad(a, widths)


def kernel(x_prompt, x_sample, c_prompt, c_sample, w_ada, b_ada, ffn1_w1, ffn1_w3, ffn1_w2,
           w_in, w_pool, s_pool, w_conv, w_out, ffn2_w1, ffn2_w3, ffn2_w2, ln_g, ln_b):
    depth = w_ada.shape[0]
    d = x_prompt.shape[-1]
    alpha = (2.0 * depth) ** 0.25
    nb_p, nb_s = c_prompt.shape[0], c_sample.shape[0]

    c_all = _pad_to(jnp.concatenate([c_prompt, c_sample], axis=0), 0, V7X_F32_SUBLANES)
    xs = [x_prompt, x_sample]
    for l in range(depth):
        mod = _ada_ln(c_all, w_ada[l], b_ada[l])
        mods = [mod[:nb_p].reshape(nb_p, N_MOD, d), mod[nb_p:nb_p + nb_s].reshape(nb_s, N_MOD, d)]

        def ffn_weights(w1, w3, w2):
            return (_pad_to(w1.astype(BF16), 1, FFN_HIDDEN), _pad_to(w3.astype(BF16), 1, FFN_HIDDEN),
                    _pad_to(w2.astype(BF16), 0, FFN_HIDDEN))

        f1 = ffn_weights(ffn1_w1[l], ffn1_w3[l], ffn1_w2[l])
        f2 = ffn_weights(ffn2_w1[l], ffn2_w3[l], ffn2_w2[l])
        w_in_b, w_pool_b, w_out_b = w_in[l].astype(BF16), w_pool[l].astype(BF16), w_out[l].astype(BF16)
        s_pool_l = s_pool[l].reshape(1, -1)
        g = [ln_g[l, i].reshape(1, d) for i in range(3)]
        bb = [ln_b[l, i].reshape(1, d) for i in range(3)]

        for i in range(2):
            x = xs[i]
            x = _ffn_sublayer(x, mods[i], *f1, g[0], bb[0], mod_base=0, alpha=alpha)
            x = _mixer_sublayer(x, mods[i], w_in_b, w_pool_b, s_pool_l, w_conv[l], w_out_b,
                                g[1], bb[1], alpha=alpha)
            x = _ffn_sublayer(x, mods[i], *f2, g[2], bb[2], mod_base=6, alpha=alpha)
            xs[i] = x
    return (xs[0], xs[1])
```
